```python
import jax, jax.numpy as jnp
from jax import lax
import numpy as np

D_MODEL = 2048
BATCH = 8
SEQ = 2048
DEPTH = 1

MEM_LEN = 256
NORM_EPS = 1e-6
GMLP_WIDTH = D_MODEL // 2
GMLP_CHUNK = 128
GMLP_GROUPS = 8
GMLP_GROUP_DIM = GMLP_WIDTH // GMLP_GROUPS
MOBA_HEAD_DIM = 128
MOBA_HEADS = (D_MODEL // 2) // MOBA_HEAD_DIM
MOBA_WIDTH = MOBA_HEADS * MOBA_HEAD_DIM
MOBA_BLOCK = 256
MOBA_TOPK = 3
MOBA_Q_CHUNK = 16
ROPE_DIM = MOBA_HEAD_DIM // 4
ROPE_THETA = 500000.0
CA_HEADS = 4
CA_HEAD_DIM = 128
CA_WIDTH = CA_HEADS * CA_HEAD_DIM
FFN_HIDDEN = -(-8 * D_MODEL // (3 * 256)) * 256
IN_COLS = 2 * GMLP_WIDTH + 3 * MOBA_WIDTH + 2 * D_MODEL
IN_SPLITS = [GMLP_WIDTH, 2 * GMLP_WIDTH, 2 * GMLP_WIDTH + MOBA_WIDTH, 2 * GMLP_WIDTH + 2 * MOBA_WIDTH,
             2 * GMLP_WIDTH + 3 * MOBA_WIDTH, 2 * GMLP_WIDTH + 3 * MOBA_WIDTH + D_MODEL]

kernel_name = "hybrid_gmlp_moba_gated_block"


def rms_norm(x, g):
    xf = x.astype(jnp.float32)
    y = xf * lax.rsqrt(jnp.mean(xf * xf, axis=-1, keepdims=True) + NORM_EPS)
    return (y * g.astype(jnp.float32)).astype(x.dtype)


def layer_norm(x, g, b):
    xf = x.astype(jnp.float32)
    mu = jnp.mean(xf, axis=-1, keepdims=True)
    xc = xf - mu
    y = xc * lax.rsqrt(jnp.mean(xc * xc, axis=-1, keepdims=True) + NORM_EPS)
    return (y * g.astype(jnp.float32) + b.astype(jnp.float32)).astype(x.dtype)


def partial_rope(x, positions):
    inv_freq = 1.0 / (ROPE_THETA ** (jnp.arange(0, ROPE_DIM, 2, dtype=jnp.float32) / ROPE_DIM))
    ang = positions.astype(jnp.float32)[..., None] * inv_freq
    cos = jnp.cos(ang)[:, :, None, :]
    sin = jnp.sin(ang)[:, :, None, :]
    xr = x[..., :ROPE_DIM].astype(jnp.float32)
    x1, x2 = xr[..., :ROPE_DIM // 2], xr[..., ROPE_DIM // 2:]
    rot = jnp.concatenate([x1 * cos - x2 * sin, x2 * cos + x1 * sin], axis=-1)
    return jnp.concatenate([rot.astype(x.dtype), x[..., ROPE_DIM:]], axis=-1)


def chunked_spatial_gating(u, v, w_s, b_s):
    B, S, _ = v.shape
    nc = S // GMLP_CHUNK
    causal = jnp.tril(jnp.ones((GMLP_CHUNK, GMLP_CHUNK), dtype=bool))
    w = jnp.where(causal[None], w_s, jnp.zeros((), w_s.dtype))
    vg = v.reshape(B, nc, GMLP_CHUNK, GMLP_GROUPS, GMLP_GROUP_DIM)
    s = jnp.einsum('gts,bcsgd->bctgd', w, vg) + b_s.T[None, None, :, :, None]
    return u * s.reshape(B, S, GMLP_WIDTH)


def moba_attention(q, k, v):
    B, S, H, Dh = q.shape
    nb = -(-S // MOBA_BLOCK)
    pad = nb * MOBA_BLOCK - S
    topk = min(MOBA_TOPK, nb)
    scale = Dh ** -0.5
    q = q.transpose(0, 2, 1, 3)
    k = jnp.pad(k.transpose(0, 2, 1, 3), ((0, 0), (0, 0), (0, pad), (0, 0)))
    v = jnp.pad(v.transpose(0, 2, 1, 3), ((0, 0), (0, 0), (0, pad), (0, 0)))
    k_blocks = k.reshape(B, H, nb, MOBA_BLOCK, Dh)
    v_blocks = v.reshape(B, H, nb, MOBA_BLOCK, Dh)
    k_mean = jnp.mean(k_blocks.astype(jnp.float32), axis=3).astype(k.dtype)
    b_idx = jnp.arange(B)[:, None, None, None]
    h_idx = jnp.arange(H)[None, :, None, None]
    neg = jnp.float32(-jnp.inf)

    def chunk(c):
        start = c * MOBA_Q_CHUNK
        qc = lax.dynamic_slice_in_dim(q, start, MOBA_Q_CHUNK, axis=2)
        q_pos = start + jnp.arange(MOBA_Q_CHUNK)
        own = start // MOBA_BLOCK
        gate = jnp.einsum('bhqd,bhnd->bhqn', qc, k_mean).astype(jnp.float32)
        gate = jnp.where((jnp.arange(nb) < own)[None, None, None, :], gate, neg)
        _, sel = lax.top_k(gate, topk)
        sel_valid = sel < own
        k_sel = k_blocks[b_idx, h_idx, sel]
        v_sel = v_blocks[b_idx, h_idx, sel]
        s_sel = jnp.einsum('bhqd,bhqjkd->bhqjk', qc, k_sel).astype(jnp.float32) * scale
        s_sel = jnp.where(sel_valid[..., None], s_sel, neg).reshape(B, H, MOBA_Q_CHUNK, topk * MOBA_BLOCK)
        k_own = lax.dynamic_slice_in_dim(k, own * MOBA_BLOCK, MOBA_BLOCK, axis=2)
        v_own = lax.dynamic_slice_in_dim(v, own * MOBA_BLOCK, MOBA_BLOCK, axis=2)
        s_own = jnp.einsum('bhqd,bhkd->bhqk', qc, k_own).astype(jnp.float32) * scale
        key_pos = own * MOBA_BLOCK + jnp.arange(MOBA_BLOCK)
        s_own = jnp.where((key_pos[None, :] <= q_pos[:, None])[None, None], s_own, neg)
        p = jax.nn.softmax(jnp.concatenate([s_sel, s_own], axis=-1), axis=-1)
        p_sel = p[..., :topk * MOBA_BLOCK].reshape(B, H, MOBA_Q_CHUNK, topk, MOBA_BLOCK).astype(v.dtype)
        p_own = p[..., topk * MOBA_BLOCK:].astype(v.dtype)
        return (jnp.einsum('bhqjk,bhqjkd->bhqd', p_sel, v_sel)
                + jnp.einsum('bhqk,bhkd->bhqd', p_own, v_own))

    out = lax.map(chunk, jnp.arange(S // MOBA_Q_CHUNK))
    return out.transpose(1, 0, 3, 2, 4).reshape(B, S, H * Dh)


def token_mixer(n, positions, w_in, ln_v_g, ln_v_b, w_spatial, b_spatial, w_branch_a, w_branch_b, w_tm_out):
    B, S, _ = n.shape
    proj = n @ w_in
    u, v, q, k, v_att, g_a, g_b = jnp.split(proj, IN_SPLITS, axis=-1)
    u = jax.nn.gelu(u, approximate=False)
    v = layer_norm(jax.nn.gelu(v, approximate=False), ln_v_g, ln_v_b)
    y_a = chunked_spatial_gating(u, v, w_spatial, b_spatial) @ w_branch_a
    q = partial_rope(q.reshape(B, S, MOBA_HEADS, MOBA_HEAD_DIM), positions)
    k = partial_rope(k.reshape(B, S, MOBA_HEADS, MOBA_HEAD_DIM), positions)
    v_att = v_att.reshape(B, S, MOBA_HEADS, MOBA_HEAD_DIM)
    y_b = moba_attention(q, k, v_att) @ w_branch_b
    merged = jax.nn.sigmoid(g_a) * y_a + jax.nn.sigmoid(g_b) * y_b
    return merged @ w_tm_out


def cross_attention(n, mem_n, w_q, w_kv, w_out):
    B, S, _ = n.shape
    M = mem_n.shape[1]
    q = (n @ w_q).reshape(B, S, CA_HEADS, CA_HEAD_DIM)
    kv = mem_n @ w_kv
    k = kv[..., :CA_WIDTH].reshape(B, M, CA_HEADS, CA_HEAD_DIM)
    v = kv[..., CA_WIDTH:].reshape(B, M, CA_HEADS, CA_HEAD_DIM)
    s = jnp.einsum('bshd,bmhd->bhsm', q, k).astype(jnp.float32) * (CA_HEAD_DIM ** -0.5)
    p = jax.nn.softmax(s, axis=-1).astype(v.dtype)
    o = jnp.einsum('bhsm,bmhd->bshd', p, v).reshape(B, S, CA_WIDTH)
    return o @ w_out


def swiglu(n, w_gate, w_up, w_down):
    return (jax.nn.silu(n @ w_gate) * (n @ w_up)) @ w_down


def setup_inputs(seed: int = 0) -> dict:
    key = jax.random.key(seed)
    ks = jax.random.split(key, 24)
    f32 = jnp.float32

    def dense(k, fan_in, shape):
        return jax.random.normal(k, shape, f32) * fan_in ** -0.5

    def gain(k, dim):
        return 1.0 + 0.02 * jax.random.normal(k, (DEPTH, dim), f32)

    x = jax.random.normal(ks[0], (BATCH, SEQ, D_MODEL), f32)
    mem = jax.random.normal(ks[1], (BATCH, MEM_LEN, D_MODEL), f32)
    positions = (jax.random.randint(ks[2], (BATCH, 1), 0, 1024, dtype=jnp.int32)
                 + jnp.arange(SEQ, dtype=jnp.int32)[None, :])
    return {
        "x": x,
        "mem": mem,
        "positions": positions,
        "tm_pre_g": gain(ks[3], D_MODEL),
        "tm_post_g": gain(ks[4], D_MODEL),
        "w_in": dense(ks[5], D_MODEL, (DEPTH, D_MODEL, IN_COLS)),
        "ln_v_g": gain(ks[6], GMLP_WIDTH),
        "ln_v_b": 0.02 * jax.random.normal(ks[7], (DEPTH, GMLP_WIDTH), f32),
        "w_spatial": dense(ks[8], GMLP_CHUNK, (DEPTH, GMLP_GROUPS, GMLP_CHUNK, GMLP_CHUNK)),
        "b_spatial": 1.0 + 0.02 * jax.random.normal(ks[9], (DEPTH, GMLP_GROUPS, GMLP_CHUNK), f32),
        "w_branch_a": dense(ks[10], GMLP_WIDTH, (DEPTH, GMLP_WIDTH, D_MODEL)),
        "w_branch_b": dense(ks[11], MOBA_WIDTH, (DEPTH, MOBA_WIDTH, D_MODEL)),
        "w_tm_out": dense(ks[12], D_MODEL, (DEPTH, D_MODEL, D_MODEL)),
        "ca_pre_g": gain(ks[13], D_MODEL),
        "ca_post_g": gain(ks[14], D_MODEL),
        "mem_norm_g": gain(ks[15], D_MODEL),
        "w_ca_q": dense(ks[16], D_MODEL, (DEPTH, D_MODEL, CA_WIDTH)),
        "w_ca_kv": dense(ks[17], D_MODEL, (DEPTH, D_MODEL, 2 * CA_WIDTH)),
        "w_ca_out": dense(ks[18], CA_WIDTH, (DEPTH, CA_WIDTH, D_MODEL)),
        "ffn_pre_g": gain(ks[19], D_MODEL),
        "ffn_post_g": gain(ks[20], D_MODEL),
        "w_ffn_gate": dense(ks[21], D_MODEL, (DEPTH, D_MODEL, FFN_HIDDEN)),
        "w_ffn_up": dense(ks[22], D_MODEL, (DEPTH, D_MODEL, FFN_HIDDEN)),
        "w_ffn_down": dense(ks[23], FFN_HIDDEN, (DEPTH, FFN_HIDDEN, D_MODEL)),
    }


def reference(x, mem, positions, tm_pre_g, tm_post_g, w_in, ln_v_g, ln_v_b, w_spatial, b_spatial,
              w_branch_a, w_branch_b, w_tm_out, ca_pre_g, ca_post_g, mem_norm_g, w_ca_q, w_ca_kv, w_ca_out,
              ffn_pre_g, ffn_post_g, w_ffn_gate, w_ffn_up, w_ffn_down):
    h = x
    for l in range(DEPTH):
        n = rms_norm(h, tm_pre_g[l])
        y = token_mixer(n, positions, w_in[l], ln_v_g[l], ln_v_b[l], w_spatial[l], b_spatial[l],
                        w_branch_a[l], w_branch_b[l], w_tm_out[l])
        h = h + rms_norm(y, tm_post_g[l])
        n = rms_norm(h, ca_pre_g[l])
        mem_n = rms_norm(mem, mem_norm_g[l])
        y = cross_attention(n, mem_n, w_ca_q[l], w_ca_kv[l], w_ca_out[l])
        h = h + rms_norm(y, ca_post_g[l])
        n = rms_norm(h, ffn_pre_g[l])
        y = swiglu(n, w_ffn_gate[l], w_ffn_up[l], w_ffn_down[l])
        h = h + rms_norm(y, ffn_post_g[l])
    return h
```

```python
import functools

import jax
import jax.numpy as jnp
from jax import lax
from jax.experimental import pallas as pl
from jax.experimental.pallas import tpu as pltpu

D_MODEL = 2048
BATCH = 8
SEQ = 2048
TOKENS = BATCH * SEQ
MEM_LEN = 256
NORM_EPS = 1e-6
GMLP_WIDTH = 1024
GMLP_CHUNK = 128
GMLP_GROUPS = 8
HEAD_DIM = 128
MOBA_HEADS = 8
MOBA_WIDTH = 1024
MOBA_BLOCK = 256
MOBA_NBLK = SEQ // MOBA_BLOCK
MOBA_TOPK = 3
ROPE_DIM = 32
ROPE_THETA = 500000.0
CA_HEADS = 4
CA_WIDTH = 512
FFN_HIDDEN = 5632
IN_COLS = 9216
SEG = 1024
N_SEG = IN_COLS // SEG
LANES = 128
NEG_BIAS = -1e9
VMEM_LIMIT = 56 * 1024 * 1024

F32 = jnp.float32
BF16 = jnp.bfloat16
_NT = (((1,), (1,)), ((), ()))


def _rms(x, g):
    return x * lax.rsqrt(jnp.mean(x * x, axis=-1, keepdims=True) + NORM_EPS) * g


def _gelu(x):
    return 0.5 * x * (1.0 + lax.erf(x * 0.7071067811865476))


def _params(sem):
    return pltpu.CompilerParams(dimension_semantics=sem, vmem_limit_bytes=VMEM_LIMIT)


def _seg_of_step(j):
    return jnp.where(j == 2, 3, jnp.where(j == 3, 2, j))


def _rope(x, pos_ref, invf_ref):
    ang = pos_ref[...].astype(F32) * invf_ref[...]
    c = jnp.cos(ang)
    s = jnp.sin(ang)
    lane = lax.broadcasted_iota(jnp.int32, ang.shape, 1)
    s_lo = jnp.where(lane < ROPE_DIM // 2, -s, 0.0)
    s_hi = jnp.where((lane >= ROPE_DIM // 2) & (lane < ROPE_DIM), s, 0.0)
    outs = []
    for h in range(MOBA_HEADS):
        xh = x[:, h * HEAD_DIM:(h + 1) * HEAD_DIM]
        up = pltpu.roll(xh, LANES - ROPE_DIM // 2, 1)
        dn = pltpu.roll(xh, ROPE_DIM // 2, 1)
        outs.append(xh * c + up * s_lo + dn * s_hi)
    return jnp.concatenate(outs, axis=1)


def _selection_bias(gate, own):
    lane = lax.broadcasted_iota(jnp.int32, gate.shape, 1)
    n = lane & (MOBA_NBLK - 1)
    rank = jnp.zeros(gate.shape, jnp.int32)
    for s in range(1, MOBA_NBLK):
        wrap = n + s >= MOBA_NBLK
        other = jnp.where(wrap, pltpu.roll(gate, MOBA_NBLK - s, 1), pltpu.roll(gate, LANES - s, 1))
        n_other = jnp.where(wrap, n + s - MOBA_NBLK, n + s)
        beats = (n_other < own) & ((other > gate) | ((other == gate) & (n_other < n)))
        rank = rank + beats.astype(jnp.int32)
    selected = (n < own) & (rank < MOBA_TOPK)
    return jnp.where(selected, 0.0, NEG_BIAS)


def _in_proj_kernel(x_ref, pos_ref, invf_ref, gpre_ref, w_ref, lng_ref, lnb_ref,
                    proj_ref, selb_ref, n_scr, kmean_scr, *, tm):
    i = pl.program_id(0)
    j = pl.program_id(1)
    nsub = tm // MOBA_BLOCK
    tiles_per_batch = SEQ // tm

    @pl.when(j == 0)
    def _():
        n_scr[...] = _rms(x_ref[...], gpre_ref[...]).astype(BF16)

    @pl.when((j == 0) & (i % tiles_per_batch == 0))
    def _():
        kmean_scr[...] = jnp.zeros_like(kmean_scr)

    acc = jnp.dot(n_scr[...], w_ref[...], preferred_element_type=F32)

    @pl.when(j == 0)
    def _():
        proj_ref[...] = _gelu(acc).astype(BF16)

    @pl.when(j == 1)
    def _():
        gl = _gelu(acc)
        xc = gl - jnp.mean(gl, axis=-1, keepdims=True)
        y = xc * lax.rsqrt(jnp.mean(xc * xc, axis=-1, keepdims=True) + NORM_EPS)
        proj_ref[...] = (y * lng_ref[...] + lnb_ref[...]).astype(BF16)

    @pl.when(j == 2)
    def _():
        kr = _rope(acc, pos_ref, invf_ref)
        proj_ref[...] = kr.astype(BF16)
        blk_row = lax.broadcasted_iota(jnp.int32, kmean_scr.shape, 0)
        km = kmean_scr[...]
        for r in range(nsub):
            blk = (i % tiles_per_batch) * nsub + r
            mean_r = jnp.mean(kr[r * MOBA_BLOCK:(r + 1) * MOBA_BLOCK], axis=0, keepdims=True)
            km = jnp.where(blk_row == blk, mean_r, km)
        kmean_scr[...] = km

    @pl.when(j == 3)
    def _():
        qr = _rope(acc, pos_ref, invf_ref)
        proj_ref[...] = qr.astype(BF16)
        km = kmean_scr[...]
        tiled = jnp.concatenate([km] * MOBA_HEADS, axis=0)
        rr = lax.broadcasted_iota(jnp.int32, tiled.shape, 0)
        cc = lax.broadcasted_iota(jnp.int32, tiled.shape, 1)
        rhs = jnp.where((rr >> 3) == (cc >> 7), tiled, 0.0)
        rhs = jnp.concatenate([rhs, jnp.zeros_like(rhs)], axis=0)
        for r in range(nsub):
            own = (i % tiles_per_batch) * nsub + r
            gate = lax.dot_general(qr[r * MOBA_BLOCK:(r + 1) * MOBA_BLOCK], rhs, _NT,
                                   precision=lax.Precision.HIGHEST, preferred_element_type=F32)
            selb_ref[r * MOBA_BLOCK:(r + 1) * MOBA_BLOCK, :] = _selection_bias(gate, own).astype(BF16)

    @pl.when(j == 4)
    def _():
        proj_ref[...] = acc.astype(BF16)

    @pl.when(j >= 5)
    def _():
        proj_ref[...] = jax.nn.sigmoid(acc).astype(BF16)


def _in_proj(x2, pos2, invf, gpre, w_in, lng, lnb, *, tm=512):
    nr = TOKENS // tm
    return pl.pallas_call(
        functools.partial(_in_proj_kernel, tm=tm),
        grid=(nr, N_SEG),
        in_specs=[
            pl.BlockSpec((tm, D_MODEL), lambda i, j: (i, 0)),
            pl.BlockSpec((tm, 1), lambda i, j: (i, 0)),
            pl.BlockSpec((1, LANES), lambda i, j: (0, 0)),
            pl.BlockSpec((1, D_MODEL), lambda i, j: (0, 0)),
            pl.BlockSpec((D_MODEL, SEG), lambda i, j: (0, _seg_of_step(j))),
            pl.BlockSpec((1, SEG), lambda i, j: (0, 0)),
            pl.BlockSpec((1, SEG), lambda i, j: (0, 0)),
        ],
        out_specs=[
            pl.BlockSpec((tm, SEG), lambda i, j: (i, _seg_of_step(j))),
            pl.BlockSpec((tm, LANES), lambda i, j: (i, 0)),
        ],
        out_shape=[
            jax.ShapeDtypeStruct((TOKENS, IN_COLS), BF16),
            jax.ShapeDtypeStruct((TOKENS, LANES), BF16),
        ],
        scratch_shapes=[
            pltpu.VMEM((tm, D_MODEL), BF16),
            pltpu.VMEM((MOBA_NBLK, MOBA_WIDTH), F32),
        ],
        compiler_params=_params(("arbitrary", "arbitrary")),
        name="in_proj",
    )(x2, pos2, invf, gpre, w_in, lng, lnb)


def _moba_kernel(q_ref, selb_ref, k_ref, v_ref, o_ref):
    i = pl.program_id(1)
    scale = HEAD_DIM ** -0.5
    row = lax.broadcasted_iota(jnp.int32, (MOBA_BLOCK, MOBA_BLOCK), 0)
    col = lax.broadcasted_iota(jnp.int32, (MOBA_BLOCK, MOBA_BLOCK), 1)
    causal = col <= row
    lane = lax.broadcasted_iota(jnp.int32, (MOBA_BLOCK, LANES), 1)
    selb = selb_ref[...]
    own_off = pl.multiple_of(i * MOBA_BLOCK, MOBA_BLOCK)

    for h in range(MOBA_HEADS):
        hs = slice(h * HEAD_DIM, (h + 1) * HEAD_DIM)
        q = q_ref[:, hs]
        s = lax.dot_general(q, k_ref[pl.ds(own_off, MOBA_BLOCK), hs], _NT,
                            preferred_element_type=F32) * scale
        s = jnp.where(causal, s, -jnp.inf)
        m = jnp.max(s, axis=1, keepdims=True)
        p = jnp.exp(s - m)
        l = jnp.sum(p, axis=1, keepdims=True)
        acc = jnp.dot(p.astype(BF16), v_ref[pl.ds(own_off, MOBA_BLOCK), hs],
                      preferred_element_type=F32)
        q_aug = jnp.concatenate([q, selb], axis=1)

        def body(j, carry, hs=hs, h=h, q_aug=q_aug):
            m, l, acc = carry
            off = pl.multiple_of(j * MOBA_BLOCK, MOBA_BLOCK)
            onehot = (lane == h * MOBA_NBLK + j).astype(BF16)
            k_aug = jnp.concatenate([k_ref[pl.ds(off, MOBA_BLOCK), hs], onehot], axis=1)
            s = lax.dot_general(q_aug, k_aug, _NT, preferred_element_type=F32) * scale
            m_new = jnp.maximum(m, jnp.max(s, axis=1, keepdims=True))
            alpha = jnp.exp(m - m_new)
            p = jnp.exp(s - m_new)
            l = alpha * l + jnp.sum(p, axis=1, keepdims=True)
            acc = alpha * acc + jnp.dot(p.astype(BF16), v_ref[pl.ds(off, MOBA_BLOCK), hs],
                                        preferred_element_type=F32)
            return m_new, l, acc

        m, l, acc = lax.fori_loop(0, i, body, (m, l, acc))
        o_ref[:, hs] = (acc / l).astype(BF16)


def _moba(proj, selb):
    qb = 2 * GMLP_WIDTH // MOBA_WIDTH
    return pl.pallas_call(
        _moba_kernel,
        grid=(BATCH, MOBA_NBLK),
        in_specs=[
            pl.BlockSpec((MOBA_BLOCK, MOBA_WIDTH), lambda b, i: (b * MOBA_NBLK + i, qb)),
            pl.BlockSpec((MOBA_BLOCK, LANES), lambda b, i: (b * MOBA_NBLK + i, 0)),
            pl.BlockSpec((SEQ, MOBA_WIDTH), lambda b, i: (b, qb + 1)),
            pl.BlockSpec((SEQ, MOBA_WIDTH), lambda b, i: (b, qb + 2)),
        ],
        out_specs=pl.BlockSpec((MOBA_BLOCK, MOBA_WIDTH), lambda b, i: (b * MOBA_NBLK + i, 0)),
        out_shape=jax.ShapeDtypeStruct((TOKENS, MOBA_WIDTH), BF16),
        compiler_params=_params(("arbitrary", "arbitrary")),
        name="moba",
    )(proj, selb, proj, proj)


def _mix_kernel(u_ref, v_ref, att_ref, sa_ref, sb_ref, wsp_ref, bsp_ref, wa_ref, wb_ref, wo_ref,
                x_ref, gpost_ref, o_ref, yin_scr, acc_scr, *, tm, nc):
    c = pl.program_id(1)

    @pl.when(c == 0)
    def _():
        r_i = lax.broadcasted_iota(jnp.int32, (GMLP_CHUNK, GMLP_CHUNK), 0)
        c_i = lax.broadcasted_iota(jnp.int32, (GMLP_CHUNK, GMLP_CHUNK), 1)
        for g in range(GMLP_GROUPS):
            gs = slice(g * LANES, (g + 1) * LANES)
            w = jnp.where(c_i <= r_i, wsp_ref[g], 0.0).astype(BF16)
            for r in range(tm // GMLP_CHUNK):
                rs = slice(r * GMLP_CHUNK, (r + 1) * GMLP_CHUNK)
                s = jnp.dot(w, v_ref[rs, gs], preferred_element_type=F32) + bsp_ref[:, gs]
                yin_scr[rs, gs] = (u_ref[rs, gs].astype(F32) * s).astype(BF16)

    ya = jnp.dot(yin_scr[...], wa_ref[...], preferred_element_type=F32)
    yb = jnp.dot(att_ref[...], wb_ref[...], preferred_element_type=F32)
    merged = (sa_ref[...].astype(F32) * ya + sb_ref[...].astype(F32) * yb).astype(BF16)
    contrib = jnp.dot(merged, wo_ref[...], preferred_element_type=F32)

    @pl.when(c == 0)
    def _():
        acc_scr[...] = contrib

    @pl.when(c > 0)
    def _():
        acc_scr[...] += contrib

    @pl.when(c == nc - 1)
    def _():
        o_ref[...] = x_ref[...] + _rms(acc_scr[...], gpost_ref[...])


def _mix(proj, att, wsp, bsp, wa, wb, wo, x2, gpost, *, tm=512, tn=512):
    nr = TOKENS // tm
    nc = D_MODEL // tn
    ga = (2 * GMLP_WIDTH + 3 * MOBA_WIDTH) // tn
    gb = ga + D_MODEL // tn
    return pl.pallas_call(
        functools.partial(_mix_kernel, tm=tm, nc=nc),
        grid=(nr, nc),
        in_specs=[
            pl.BlockSpec((tm, GMLP_WIDTH), lambda i, c: (i, 0)),
            pl.BlockSpec((tm, GMLP_WIDTH), lambda i, c: (i, 1)),
            pl.BlockSpec((tm, MOBA_WIDTH), lambda i, c: (i, 0)),
            pl.BlockSpec((tm, tn), lambda i, c: (i, ga + c)),
            pl.BlockSpec((tm, tn), lambda i, c: (i, gb + c)),
            pl.BlockSpec((GMLP_GROUPS, GMLP_CHUNK, GMLP_CHUNK), lambda i, c: (0, 0, 0)),
            pl.BlockSpec((GMLP_CHUNK, GMLP_WIDTH), lambda i, c: (0, 0)),
            pl.BlockSpec((GMLP_WIDTH, tn), lambda i, c: (0, c)),
            pl.BlockSpec((MOBA_WIDTH, tn), lambda i, c: (0, c)),
            pl.BlockSpec((tn, D_MODEL), lambda i, c: (c, 0)),
            pl.BlockSpec((tm, D_MODEL), lambda i, c: (i, 0)),
            pl.BlockSpec((1, D_MODEL), lambda i, c: (0, 0)),
        ],
        out_specs=pl.BlockSpec((tm, D_MODEL), lambda i, c: (i, 0)),
        out_shape=jax.ShapeDtypeStruct((TOKENS, D_MODEL), F32),
        scratch_shapes=[
            pltpu.VMEM((tm, GMLP_WIDTH), BF16),
            pltpu.VMEM((tm, D_MODEL), F32),
        ],
        compiler_params=_params(("arbitrary", "arbitrary")),
        name="mix",
    )(proj, proj, att, proj, proj, wsp, bsp, wa, wb, wo, x2, gpost)


def _mem_kv_kernel(mem_ref, g_ref, w_ref, o_ref):
    n = _rms(mem_ref[...], g_ref[...]).astype(BF16)
    o_ref[...] = jnp.dot(n, w_ref[...], preferred_element_type=F32).astype(BF16)


def _mem_kv(mem2, g, w_kv):
    return pl.pallas_call(
        _mem_kv_kernel,
        grid=(BATCH,),
        in_specs=[
            pl.BlockSpec((MEM_LEN, D_MODEL), lambda b: (b, 0)),
            pl.BlockSpec((1, D_MODEL), lambda b: (0, 0)),
            pl.BlockSpec((D_MODEL, 2 * CA_WIDTH), lambda b: (0, 0)),
        ],
        out_specs=pl.BlockSpec((MEM_LEN, 2 * CA_WIDTH), lambda b: (b, 0)),
        out_shape=jax.ShapeDtypeStruct((BATCH * MEM_LEN, 2 * CA_WIDTH), BF16),
        compiler_params=_params(("arbitrary",)),
        name="mem_kv",
    )(mem2, g, w_kv)


def _cross_kernel(h_ref, gpre_ref, wq_ref, kv_ref, wo_ref, gpost_ref, o_ref):
    scale = HEAD_DIM ** -0.5
    h = h_ref[...]
    n = _rms(h, gpre_ref[...]).astype(BF16)
    q = jnp.dot(n, wq_ref[...], preferred_element_type=F32).astype(BF16)
    outs = []
    for hd in range(CA_HEADS):
        ks = slice(hd * HEAD_DIM, (hd + 1) * HEAD_DIM)
        vs = slice(CA_WIDTH + hd * HEAD_DIM, CA_WIDTH + (hd + 1) * HEAD_DIM)
        s = lax.dot_general(q[:, ks], kv_ref[:, ks], _NT, preferred_element_type=F32) * scale
        m = jnp.max(s, axis=1, keepdims=True)
        p = jnp.exp(s - m)
        p = p / jnp.sum(p, axis=1, keepdims=True)
        outs.append(jnp.dot(p.astype(BF16), kv_ref[:, vs], preferred_element_type=F32).astype(BF16))
    o = jnp.concatenate(outs, axis=1)
    y = jnp.dot(o, wo_ref[...], preferred_element_type=F32)
    o_ref[...] = h + _rms(y, gpost_ref[...])


def _cross(h1, gpre, wq, kv, wo, gpost, *, tm=512):
    nr = TOKENS // tm
    tiles_per_batch = SEQ // tm
    return pl.pallas_call(
        _cross_kernel,
        grid=(nr,),
        in_specs=[
            pl.BlockSpec((tm, D_MODEL), lambda i: (i, 0)),
            pl.BlockSpec((1, D_MODEL), lambda i: (0, 0)),
            pl.BlockSpec((D_MODEL, CA_WIDTH), lambda i: (0, 0)),
            pl.BlockSpec((MEM_LEN, 2 * CA_WIDTH), lambda i: (i // tiles_per_batch, 0)),
            pl.BlockSpec((CA_WIDTH, D_MODEL), lambda i: (0, 0)),
            pl.BlockSpec((1, D_MODEL), lambda i: (0, 0)),
        ],
        out_specs=pl.BlockSpec((tm, D_MODEL), lambda i: (i, 0)),
        out_shape=jax.ShapeDtypeStruct((TOKENS, D_MODEL), F32),
        compiler_params=_params(("arbitrary",)),
        name="cross",
    )(h1, gpre, wq, kv, wo, gpost)


def _ffn_kernel(h_ref, gpre_ref, wg_ref, wu_ref, wd_ref, gpost_ref, o_ref, n_scr, acc_scr, *, nk):
    k = pl.program_id(1)

    @pl.when(k == 0)
    def _():
        n_scr[...] = _rms(h_ref[...], gpre_ref[...]).astype(BF16)

    n = n_scr[...]
    g = jnp.dot(n, wg_ref[...], preferred_element_type=F32)
    u = jnp.dot(n, wu_ref[...], preferred_element_type=F32)
    a = (g * jax.nn.sigmoid(g) * u).astype(BF16)
    contrib = jnp.dot(a, wd_ref[...], preferred_element_type=F32)

    @pl.when(k == 0)
    def _():
        acc_scr[...] = contrib

    @pl.when(k > 0)
    def _():
        acc_scr[...] += contrib

    @pl.when(k == nk - 1)
    def _():
        o_ref[...] = h_ref[...] + _rms(acc_scr[...], gpost_ref[...])


def _ffn(h2, gpre, wg, wu, wd, gpost, *, tm=512, th=512):
    nr = TOKENS // tm
    nk = FFN_HIDDEN // th
    return pl.pallas_call(
        functools.partial(_ffn_kernel, nk=nk),
        grid=(nr, nk),
        in_specs=[
            pl.BlockSpec((tm, D_MODEL), lambda i, k: (i, 0)),
            pl.BlockSpec((1, D_MODEL), lambda i, k: (0, 0)),
            pl.BlockSpec((D_MODEL, th), lambda i, k: (0, k)),
            pl.BlockSpec((D_MODEL, th), lambda i, k: (0, k)),
            pl.BlockSpec((th, D_MODEL), lambda i, k: (k, 0)),
            pl.BlockSpec((1, D_MODEL), lambda i, k: (0, 0)),
        ],
        out_specs=pl.BlockSpec((tm, D_MODEL), lambda i, k: (i, 0)),
        out_shape=jax.ShapeDtypeStruct((TOKENS, D_MODEL), F32),
        scratch_shapes=[
            pltpu.VMEM((tm, D_MODEL), BF16),
            pltpu.VMEM((tm, D_MODEL), F32),
        ],
        compiler_params=_params(("arbitrary", "arbitrary")),
        name="ffn",
    )(h2, gpre, wg, wu, wd, gpost)


def kernel(x, mem, positions, tm_pre_g, tm_post_g, w_in, ln_v_g, ln_v_b, w_spatial, b_spatial,
           w_branch_a, w_branch_b, w_tm_out, ca_pre_g, ca_post_g, mem_norm_g, w_ca_q, w_ca_kv, w_ca_out,
           ffn_pre_g, ffn_post_g, w_ffn_gate, w_ffn_up, w_ffn_down):
    assert x.shape == (BATCH, SEQ, D_MODEL) and mem.shape == (BATCH, MEM_LEN, D_MODEL)
    assert tm_pre_g.shape[0] == 1, "one layer"
    x2 = x.reshape(TOKENS, D_MODEL)
    mem2 = mem.reshape(BATCH * MEM_LEN, D_MODEL)
    pos2 = positions.reshape(TOKENS, 1)
    inv_freq = 1.0 / (ROPE_THETA ** (jnp.arange(0, ROPE_DIM, 2, dtype=F32) / ROPE_DIM))
    invf = jnp.concatenate([inv_freq, inv_freq, jnp.zeros((LANES - ROPE_DIM,), F32)]).reshape(1, LANES)
    bsp = jnp.repeat(b_spatial[0].T, GMLP_WIDTH // GMLP_GROUPS, axis=1)

    proj, selb = _in_proj(x2, pos2, invf, tm_pre_g, w_in[0].astype(BF16), ln_v_g, ln_v_b)
    att = _moba(proj, selb)
    h1 = _mix(proj, att, w_spatial[0], bsp, w_branch_a[0].astype(BF16), w_branch_b[0].astype(BF16),
              w_tm_out[0].astype(BF16), x2, tm_post_g)
    kv = _mem_kv(mem2, mem_norm_g, w_ca_kv[0].astype(BF16))
    h2 = _cross(h1, ca_pre_g, w_ca_q[0].astype(BF16), kv, w_ca_out[0].astype(BF16), ca_post_g)
    out = _ffn(h2, ffn_pre_g, w_ffn_gate[0].astype(BF16), w_ffn_up[0].astype(BF16),
               w_ffn_down[0].astype(BF16), ffn_post_g)
    return out.reshape(BATCH, SEQ, D_MODEL)
```
